```python
import math
import jax, jax.numpy as jnp
from jax import lax
import numpy as np

D_MODEL = 2048
BATCH = 4
SEQ = 2048
DEPTH = 4
DEC_BATCH = 128
DEC_SEQ = 4
PAST_LEN = 16384
PAGE_SIZE = 128

N_MIXERS = 2
N_A = (DEPTH + 1) // 2
N_B = DEPTH // 2
SGU_CHUNK = 128
SGU_GROUPS = 8
D_SGU = 3 * D_MODEL
SGU_GROUP_DIM = D_SGU // SGU_GROUPS
RET_HEADS = 8
RET_DK = D_MODEL // RET_HEADS
RET_DV = 2 * D_MODEL // RET_HEADS
RET_CHUNK = 128
ROPE_BASE = 10000.0
D_FF = 5632
N_ADA = 9
EPS = 1e-6

kernel_name = "hybrid_sgu_retention_macaron_adaln_decoder_step"


def _rmsnorm(x, g):
    xf = x.astype(jnp.float32)
    y = xf * lax.rsqrt(jnp.mean(xf * xf, axis=-1, keepdims=True) + EPS)
    return (y * g.astype(jnp.float32)).astype(x.dtype)


def _layernorm(x, g, b):
    xf = x.astype(jnp.float32)
    mu = jnp.mean(xf, axis=-1, keepdims=True)
    xc = xf - mu
    y = xc * lax.rsqrt(jnp.mean(xc * xc, axis=-1, keepdims=True) + EPS)
    return (y * g.astype(jnp.float32) + b.astype(jnp.float32)).astype(x.dtype)


def _modulate(h, shift, scale):
    return h * (1.0 + scale[:, None, :]) + shift[:, None, :]


def _swiglu(h, w_gu, w_down):
    gt, up = jnp.split(h @ w_gu, 2, axis=-1)
    return (jax.nn.silu(gt) * up) @ w_down


def _sgu_mixer(h, w_in, ln_g, ln_b, w_s, b_s, w_out):
    B, T, _ = h.shape
    z = jax.nn.gelu(h @ w_in)
    u, v = jnp.split(z, 2, axis=-1)
    vn = _layernorm(v, ln_g, ln_b)
    C = min(SGU_CHUNK, T)
    N = T // C
    mask = jnp.tril(jnp.ones((C, C), dtype=bool))
    wm = jnp.where(mask[None], w_s[:, :C, :C], 0.0).astype(vn.dtype)
    vg = vn.reshape(B, N, C, SGU_GROUPS, SGU_GROUP_DIM)
    mixed = jnp.einsum('gts,bnsgd->bntgd', wm, vg) + b_s[:, :C].T[None, None, :, :, None]
    y = (u * mixed.reshape(B, T, D_SGU)) @ w_out
    return y, vn


def _rotate_every_two(x):
    x1 = x[..., 0::2]
    x2 = x[..., 1::2]
    return jnp.stack((-x2, x1), axis=-1).reshape(x.shape)


def _rotary(x, pos):
    theta = 1.0 / (ROPE_BASE ** jnp.linspace(0.0, 1.0, RET_DK // 2, dtype=jnp.float32))
    ang = pos[:, None] * theta[None, :]
    sin = jnp.repeat(jnp.sin(ang), 2, axis=-1)[None, :, None, :]
    cos = jnp.repeat(jnp.cos(ang), 2, axis=-1)[None, :, None, :]
    xf = x.astype(jnp.float32)
    return xf * cos + _rotate_every_two(xf) * sin


def _chunk_retention(q, k, v, s0):
    B, T, H, _ = q.shape
    C = math.gcd(T, RET_CHUNK)
    N = T // C
    lg = jnp.log(1.0 - 2.0 ** (-5.0 - jnp.arange(H, dtype=jnp.float32)))
    idx = jnp.arange(C, dtype=jnp.float32)
    diff = idx[:, None] - idx[None, :]
    dmask = jnp.where(diff[None] >= 0, jnp.exp(lg[:, None, None] * jnp.maximum(diff, 0.0)[None]), 0.0)
    q_dec = jnp.exp(lg[:, None] * (idx + 1.0))[None, :, :, None]
    k_dec = jnp.exp(lg[:, None] * (C - 1.0 - idx))[None, :, :, None]
    c_dec = jnp.exp(lg * C)[None, :, None, None]

    def to_chunks(a):
        return a.astype(jnp.float32).reshape(B, N, C, H, a.shape[-1]).transpose(1, 0, 3, 2, 4)

    def step(s, qkv):
        qc, kc, vc = qkv
        scores = jnp.einsum('bhqd,bhkd->bhqk', qc, kc) * dmask[None]
        o = jnp.einsum('bhqk,bhke->bhqe', scores, vc) + jnp.einsum('bhqd,bhde->bhqe', qc, s) * q_dec
        s = s * c_dec + jnp.einsum('bhkd,bhke->bhde', kc * k_dec, vc)
        return s, o

    s, o = lax.scan(step, s0.astype(jnp.float32), (to_chunks(q), to_chunks(k), to_chunks(v)))
    o = o.transpose(1, 0, 3, 2, 4).reshape(B, T, H, v.shape[-1])
    return o, s


def _retention_mixer(h, s0, pos0, w_in, w_out):
    B, T, _ = h.shape
    p = h @ w_in
    q, k, v, g = jnp.split(p, [D_MODEL, 2 * D_MODEL, 4 * D_MODEL], axis=-1)
    pos = pos0 + jnp.arange(T, dtype=jnp.float32)
    q = _rotary(q.reshape(B, T, RET_HEADS, RET_DK), pos)
    k = _rotary(k.reshape(B, T, RET_HEADS, RET_DK), pos) * (RET_DK ** -0.5)
    v = v.reshape(B, T, RET_HEADS, RET_DV)
    o, s = _chunk_retention(q, k, v, s0)
    o = o * lax.rsqrt(jnp.mean(o * o, axis=-1, keepdims=True) + EPS)
    y = (jax.nn.silu(g) * o.reshape(B, T, RET_HEADS * RET_DV).astype(h.dtype)) @ w_out
    return y, s.astype(h.dtype)


def _trunk(x, c, ret_s0, pos0, norm_g, w_ada, b_ada, ffn1_w_gu, ffn1_w_down, ffn2_w_gu, ffn2_w_down,
           sgu_w_in, sgu_ln_g, sgu_ln_b, sgu_w_s, sgu_b_s, sgu_w_out, ret_w_in, ret_w_out,
           final_norm_g, w_ada_f, b_ada_f):
    B = x.shape[0]
    cs = jax.nn.silu(c)
    ret_states, sgu_vs = [], []
    for i in range(DEPTH):
        mod = cs @ w_ada[i] + b_ada[i]
        sh1, sc1, g1, shm, scm, gm, sh2, sc2, g2 = jnp.split(mod, N_ADA, axis=-1)
        h = _modulate(_rmsnorm(x, norm_g[i, 0]), sh1, sc1)
        x = x + 0.5 * (1.0 + g1)[:, None, :] * _swiglu(h, ffn1_w_gu[i], ffn1_w_down[i])
        h = _modulate(_rmsnorm(x, norm_g[i, 1]), shm, scm)
        j = i // N_MIXERS
        if i % N_MIXERS == 0:
            y, vn = _sgu_mixer(h, sgu_w_in[j], sgu_ln_g[j], sgu_ln_b[j], sgu_w_s[j], sgu_b_s[j], sgu_w_out[j])
            sgu_vs.append(vn)
        else:
            s0 = jnp.zeros((B, RET_HEADS, RET_DK, RET_DV), jnp.float32) if ret_s0 is None else ret_s0[j]
            y, s = _retention_mixer(h, s0, pos0, ret_w_in[j], ret_w_out[j])
            ret_states.append(s)
        x = x + (1.0 + gm)[:, None, :] * y
        h = _modulate(_rmsnorm(x, norm_g[i, 2]), sh2, sc2)
        x = x + 0.5 * (1.0 + g2)[:, None, :] * _swiglu(h, ffn2_w_gu[i], ffn2_w_down[i])
    shf, scf = jnp.split(cs @ w_ada_f + b_ada_f, 2, axis=-1)
    y = _modulate(_rmsnorm(x, final_norm_g), shf, scf)
    return y, ret_states, sgu_vs


def setup_inputs(seed: int = 0) -> dict:
    key = jax.random.key(seed)
    ks = jax.random.split(key, 32)
    f32 = jnp.float32
    D, F = D_MODEL, D_FF
    nrm = lambda k, shape, s: jax.random.normal(k, shape, f32) * s
    return {
        "x_prompt": nrm(ks[0], (BATCH, SEQ, D), 1.0),
        "x_sample": nrm(ks[1], (DEC_BATCH, DEC_SEQ, D), 1.0),
        "state_ret": nrm(ks[2], (N_B, DEC_BATCH, RET_HEADS, RET_DK, RET_DV), 1.0),
        "c_prompt": nrm(ks[3], (BATCH, D), 1.0),
        "c_sample": nrm(ks[4], (DEC_BATCH, D), 1.0),
        "norm_g": 1.0 + nrm(ks[5], (DEPTH, 3, D), 0.02),
        "w_ada": nrm(ks[6], (DEPTH, D, N_ADA * D), 0.3 * D ** -0.5),
        "b_ada": nrm(ks[7], (DEPTH, N_ADA * D), 0.02),
        "ffn1_w_gu": nrm(ks[8], (DEPTH, D, 2 * F), D ** -0.5),
        "ffn1_w_down": nrm(ks[9], (DEPTH, F, D), F ** -0.5),
        "ffn2_w_gu": nrm(ks[10], (DEPTH, D, 2 * F), D ** -0.5),
        "ffn2_w_down": nrm(ks[11], (DEPTH, F, D), F ** -0.5),
        "sgu_w_in": nrm(ks[12], (N_A, D, 2 * D_SGU), D ** -0.5),
        "sgu_ln_g": 1.0 + nrm(ks[13], (N_A, D_SGU), 0.02),
        "sgu_ln_b": nrm(ks[14], (N_A, D_SGU), 0.02),
        "sgu_w_s": nrm(ks[15], (N_A, SGU_GROUPS, SGU_CHUNK, SGU_CHUNK), SGU_CHUNK ** -0.5),
        "sgu_b_s": 1.0 + nrm(ks[16], (N_A, SGU_GROUPS, SGU_CHUNK), 0.02),
        "sgu_w_out": nrm(ks[17], (N_A, D_SGU, D), D_SGU ** -0.5),
        "ret_w_in": nrm(ks[18], (N_B, D, 6 * D), D ** -0.5),
        "ret_w_out": nrm(ks[19], (N_B, RET_HEADS * RET_DV, D), (RET_HEADS * RET_DV) ** -0.5),
        "final_norm_g": 1.0 + nrm(ks[20], (D,), 0.02),
        "w_ada_f": nrm(ks[21], (D, 2 * D), 0.3 * D ** -0.5),
        "b_ada_f": nrm(ks[22], (2 * D,), 0.02),
    }


def reference(x_prompt, x_sample, state_ret, c_prompt, c_sample, norm_g, w_ada, b_ada,
              ffn1_w_gu, ffn1_w_down, ffn2_w_gu, ffn2_w_down, sgu_w_in, sgu_ln_g, sgu_ln_b,
              sgu_w_s, sgu_b_s, sgu_w_out, ret_w_in, ret_w_out, final_norm_g, w_ada_f, b_ada_f):
    weights = (norm_g, w_ada, b_ada, ffn1_w_gu, ffn1_w_down, ffn2_w_gu, ffn2_w_down,
               sgu_w_in, sgu_ln_g, sgu_ln_b, sgu_w_s, sgu_b_s, sgu_w_out, ret_w_in, ret_w_out,
               final_norm_g, w_ada_f, b_ada_f)
    y_prompt, ret_p, _ = _trunk(x_prompt, c_prompt, None, 0.0, *weights)
    y_sample, ret_s, sgu_s = _trunk(x_sample, c_sample, state_ret, float(PAST_LEN), *weights)
    state_ret_prompt = jnp.stack(ret_p)
    state_ret_sample = jnp.stack(ret_s)
    state_sgu_v_sample = jnp.stack(sgu_s)
    return (y_prompt, y_sample, state_ret_prompt, state_ret_sample, state_sgu_v_sample)
```

```python
import functools
import math

import jax
import jax.numpy as jnp
from jax import lax
from jax.experimental import pallas as pl
from jax.experimental.pallas import tpu as pltpu

F32 = jnp.float32
BF16 = jnp.bfloat16

D_MODEL = 2048
BATCH = 4
SEQ = 2048
DEPTH = 4
DEC_BATCH = 128
DEC_SEQ = 4
PAST_LEN = 16384
N_A = 2
N_B = 2
SGU_CHUNK = 128
SGU_GROUPS = 8
D_SGU = 3 * D_MODEL
SGU_GROUP_DIM = D_SGU // SGU_GROUPS
RET_HEADS = 8
RET_DK = D_MODEL // RET_HEADS
RET_DV = 2 * D_MODEL // RET_HEADS
RET_CHUNK = 128
ROPE_BASE = 10000.0
D_FF = 5632
N_ADA = 9
EPS = 1e-6

M_P = BATCH * SEQ
M_S = DEC_BATCH * DEC_SEQ
M_ALL = M_P + M_S
TM = 512
N_MT = M_ALL // TM
TILES_PER_PROMPT_SEQ = SEQ // TM
CHUNK = 128
N_CHUNKS_P = M_P // CHUNK
N_CHUNKS_S = M_S // CHUNK
N_CHUNKS = N_CHUNKS_P + N_CHUNKS_S
CHUNKS_PER_SEQ = SEQ // CHUNK
C_PAD = 144
S_PAD = 16
CAST_ROWS = 256
VMEM_LIMIT = 58 * 1024 * 1024


def _cparams(n_axes):
    return pltpu.CompilerParams(
        dimension_semantics=("arbitrary",) * n_axes, vmem_limit_bytes=VMEM_LIMIT)


def _gelu_tanh(x):
    return jax.nn.gelu(x, approximate=True)


def _ada_kernel(c_ref, w_ref, b_ref, o_ref):
    a = jax.nn.silu(c_ref[...]).astype(BF16)
    w = w_ref[...].astype(BF16)
    o_ref[...] = jnp.dot(a, w, preferred_element_type=F32) + b_ref[...]


def _ada(c_pad, w, b, tn):
    n_l, k, n = w.shape
    return pl.pallas_call(
        _ada_kernel,
        grid=(n_l, n // tn),
        in_specs=[
            pl.BlockSpec((C_PAD, k), lambda l, j: (0, 0)),
            pl.BlockSpec((None, k, tn), lambda l, j: (l, 0, j)),
            pl.BlockSpec((None, 1, tn), lambda l, j: (l, 0, j)),
        ],
        out_specs=pl.BlockSpec((None, C_PAD, tn), lambda l, j: (l, 0, j)),
        out_shape=jax.ShapeDtypeStruct((n_l, C_PAD, n), F32),
        compiler_params=_cparams(2),
        name="ada",
    )(c_pad, w, b.reshape(n_l, 1, n))


def _prompt_seq(i):
    return jnp.minimum(i // TILES_PER_PROMPT_SEQ, BATCH - 1)


def _norm_kernel(x_ref, g_ref, shp_ref, scp_ref, shs_ref, scs_ref, o_ref):
    x = x_ref[...]
    y = x * lax.rsqrt(jnp.mean(x * x, axis=-1, keepdims=True) + EPS)
    y = y * g_ref[...]
    is_s = pl.program_id(0) == N_MT - 1
    shift = jnp.where(is_s, shs_ref[...], shp_ref[...])
    scale = jnp.where(is_s, scs_ref[...], scp_ref[...])
    o_ref[...] = (y * (1.0 + scale) + shift).astype(o_ref.dtype)


def _norm_mod(x, gains, g_idx, mod_p, mod_s, l, c_shift, c_scale, out_dtype):
    d = D_MODEL
    return pl.pallas_call(
        _norm_kernel,
        grid=(N_MT,),
        in_specs=[
            pl.BlockSpec((TM, d), lambda i: (i, 0)),
            pl.BlockSpec((None, 1, d), lambda i: (g_idx, 0, 0)),
            pl.BlockSpec((None, None, 1, d), lambda i: (l, _prompt_seq(i), 0, c_shift)),
            pl.BlockSpec((None, None, 1, d), lambda i: (l, _prompt_seq(i), 0, c_scale)),
            pl.BlockSpec((None, TM, d), lambda i: (l, 0, c_shift)),
            pl.BlockSpec((None, TM, d), lambda i: (l, 0, c_scale)),
        ],
        out_specs=pl.BlockSpec((TM, d), lambda i: (i, 0)),
        out_shape=jax.ShapeDtypeStruct((M_ALL, d), out_dtype),
        compiler_params=_cparams(1),
        name="norm_mod",
    )(x, gains, mod_p, mod_p, mod_s, mod_s)


def _round_weights(w_refs, wb_refs):
    @pl.when(pl.program_id(1) == 0)
    def _():
        for w_ref, wb_ref in zip(w_refs, wb_refs):
            def body(c, carry, w_ref=w_ref, wb_ref=wb_ref):
                r = pl.multiple_of(c * CAST_ROWS, CAST_ROWS)
                wb_ref[pl.ds(r, CAST_ROWS), :] = w_ref[pl.ds(r, CAST_ROWS), :].astype(BF16)
                return carry
            lax.fori_loop(0, w_ref.shape[0] // CAST_ROWS, body, 0)


def _mm_act_kernel(a_ref, w_ref, o_ref, wb_ref, *, act):
    _round_weights([w_ref], [wb_ref])
    y = jnp.dot(a_ref[...], wb_ref[...], preferred_element_type=F32)
    if act is not None:
        y = act(y)
    o_ref[...] = y.astype(o_ref.dtype)


def _mm_swiglu_kernel(a_ref, wg_ref, wu_ref, o_ref, wgb_ref, wub_ref):
    _round_weights([wg_ref, wu_ref], [wgb_ref, wub_ref])
    a = a_ref[...]
    gt = jnp.dot(a, wgb_ref[...], preferred_element_type=F32)
    up = jnp.dot(a, wub_ref[...], preferred_element_type=F32)
    o_ref[...] = (jax.nn.silu(gt) * up).astype(o_ref.dtype)


def _mm_res_kernel(a_ref, w_ref, x_ref, gp_ref, gs_ref, o_ref, wb_ref, *, coef):
    _round_weights([w_ref], [wb_ref])
    y = jnp.dot(a_ref[...], wb_ref[...], preferred_element_type=F32)
    is_s = pl.program_id(1) == N_MT - 1
    gate = jnp.where(is_s, gs_ref[...], gp_ref[...])
    o_ref[...] = x_ref[...] + (coef * (1.0 + gate)) * y


def _mm_act(a, w, l, col0, n_cols, tn, act, out_dtype, name):
    k = a.shape[1]
    return pl.pallas_call(
        functools.partial(_mm_act_kernel, act=act),
        grid=(n_cols // tn, N_MT),
        in_specs=[
            pl.BlockSpec((TM, k), lambda j, i: (i, 0)),
            pl.BlockSpec((None, k, tn), lambda j, i: (l, 0, col0 + j)),
        ],
        out_specs=pl.BlockSpec((TM, tn), lambda j, i: (i, j)),
        out_shape=jax.ShapeDtypeStruct((M_ALL, n_cols), out_dtype),
        scratch_shapes=[pltpu.VMEM((k, tn), BF16)],
        compiler_params=_cparams(2),
        name=name,
    )(a, w)


def _mm_swiglu(a, w_gu, l, tf):
    k = a.shape[1]
    n_f = D_FF // tf
    return pl.pallas_call(
        _mm_swiglu_kernel,
        grid=(n_f, N_MT),
        in_specs=[
            pl.BlockSpec((TM, k), lambda j, i: (i, 0)),
            pl.BlockSpec((None, k, tf), lambda j, i: (l, 0, j)),
            pl.BlockSpec((None, k, tf), lambda j, i: (l, 0, n_f + j)),
        ],
        out_specs=pl.BlockSpec((TM, tf), lambda j, i: (i, j)),
        out_shape=jax.ShapeDtypeStruct((M_ALL, D_FF), BF16),
        scratch_shapes=[pltpu.VMEM((k, tf), BF16), pltpu.VMEM((k, tf), BF16)],
        compiler_params=_cparams(2),
        name="ffn_gu",
    )(a, w_gu, w_gu)


def _mm_res(a, w, wl, x, mod_p, mod_s, l, c_gate, coef, tn, name):
    k = a.shape[1]
    d = D_MODEL
    nt = d // tn
    return pl.pallas_call(
        functools.partial(_mm_res_kernel, coef=coef),
        grid=(nt, N_MT),
        in_specs=[
            pl.BlockSpec((TM, k), lambda j, i: (i, 0)),
            pl.BlockSpec((None, k, tn), lambda j, i: (wl, 0, j), pipeline_mode=pl.Buffered(1)),
            pl.BlockSpec((TM, tn), lambda j, i: (i, j)),
            pl.BlockSpec((None, None, 1, tn), lambda j, i: (l, _prompt_seq(i), 0, c_gate * nt + j)),
            pl.BlockSpec((None, TM, tn), lambda j, i: (l, 0, c_gate * nt + j)),
        ],
        out_specs=pl.BlockSpec((TM, tn), lambda j, i: (i, j)),
        out_shape=jax.ShapeDtypeStruct((M_ALL, d), F32),
        scratch_shapes=[pltpu.VMEM((k, tn), BF16)],
        compiler_params=_cparams(2),
        name=name,
    )(a, w, x, mod_p, mod_s)


def _is_sample_chunk(i):
    return (i >= N_CHUNKS_P).astype(jnp.int32)


def _sgu_kernel(u_ref, v_ref, lg_ref, lb_ref, wm_ref, bias_ref, o_ref, vn_ref):
    v = v_ref[...]
    mu = jnp.mean(v, axis=-1, keepdims=True)
    vc = v - mu
    vn = vc * lax.rsqrt(jnp.mean(vc * vc, axis=-1, keepdims=True) + EPS)
    vn = vn * lg_ref[...] + lb_ref[...]
    vn_ref[...] = vn
    vnb = vn.astype(BF16)
    row = lax.broadcasted_iota(jnp.int32, (CHUNK, CHUNK), 0)
    col = lax.broadcasted_iota(jnp.int32, (CHUNK, CHUNK), 1)
    causal = row >= col
    for g in range(SGU_GROUPS):
        sl = slice(g * SGU_GROUP_DIM, (g + 1) * SGU_GROUP_DIM)
        wm = jnp.where(causal, wm_ref[g], 0.0).astype(BF16)
        mixed = jnp.dot(wm, vnb[:, sl], preferred_element_type=F32) + bias_ref[:, sl]
        o_ref[:, sl] = (u_ref[:, sl].astype(F32) * mixed).astype(BF16)


def _sgu_mix(u, v, ln_g, ln_b, j, wm_all, bias_all):
    n = D_SGU
    return pl.pallas_call(
        _sgu_kernel,
        grid=(N_CHUNKS,),
        in_specs=[
            pl.BlockSpec((CHUNK, n), lambda i: (i, 0)),
            pl.BlockSpec((CHUNK, n), lambda i: (i, 0)),
            pl.BlockSpec((None, 1, n), lambda i: (j, 0, 0)),
            pl.BlockSpec((None, 1, n), lambda i: (j, 0, 0)),
            pl.BlockSpec((None, SGU_GROUPS, CHUNK, CHUNK), lambda i: (_is_sample_chunk(i), 0, 0, 0)),
            pl.BlockSpec((None, CHUNK, n), lambda i: (_is_sample_chunk(i), 0, 0)),
        ],
        out_specs=[
            pl.BlockSpec((CHUNK, n), lambda i: (i, 0)),
            pl.BlockSpec((CHUNK, n), lambda i: (jnp.maximum(i - N_CHUNKS_P, 0), 0)),
        ],
        out_shape=[
            jax.ShapeDtypeStruct((M_ALL, n), BF16),
            jax.ShapeDtypeStruct((M_S, n), F32),
        ],
        compiler_params=_cparams(1),
        name="sgu_mix",
    )(u, v, ln_g, ln_b, wm_all, bias_all)


def _rotary(x, cos, sin_signed):
    w = x.shape[-1]
    lane = lax.broadcasted_iota(jnp.int32, x.shape, x.ndim - 1)
    nxt = pltpu.roll(x, w - 1, x.ndim - 1)
    prv = pltpu.roll(x, 1, x.ndim - 1)
    swapped = jnp.where((lane & 1) == 0, nxt, prv)
    return x * cos + swapped * sin_signed


def _ret_sample_kernel(cd_ref, s0_ref, q_ref, k_ref, v_ref, cos_ref, sin_ref, qd_ref, kd_ref,
                       oc_ref, sn_ref):
    cos = cos_ref[...]
    sin = sin_ref[...]
    for h in range(RET_HEADS):
        ks = slice(h * RET_DK, (h + 1) * RET_DK)
        vs = slice(h * RET_DV, (h + 1) * RET_DV)
        s0 = s0_ref[h]
        q = _rotary(q_ref[:, ks], cos, sin)
        oc = jnp.dot(q.astype(BF16), s0.astype(BF16), preferred_element_type=F32) * qd_ref[h]
        oc_ref[:, vs] = oc[:DEC_SEQ]
        k = _rotary(k_ref[:, ks], cos, sin) * (RET_DK ** -0.5)
        kd = (k * kd_ref[h]).astype(BF16)
        kv = lax.dot_general(kd, v_ref[:, vs], (((0,), (0,)), ((), ())), preferred_element_type=F32)
        sn_ref[h] = s0 * cd_ref[h] + kv


def _ret_sample(cdec, state_all, j, q16, k16, v16, cos16, sin16, qdec16, kdec16, sn_prev):
    st_block = (None, None, RET_HEADS, RET_DK, RET_DV)
    in_specs = [
        pl.BlockSpec(memory_space=pltpu.SMEM),
        pl.BlockSpec(st_block, lambda b: (j, b, 0, 0, 0)),
        pl.BlockSpec((None, S_PAD, D_MODEL), lambda b: (b, 0, 0)),
        pl.BlockSpec((None, S_PAD, D_MODEL), lambda b: (b, 0, 0)),
        pl.BlockSpec((None, S_PAD, 2 * D_MODEL), lambda b: (b, 0, 0)),
        pl.BlockSpec((S_PAD, RET_DK), lambda b: (0, 0)),
        pl.BlockSpec((S_PAD, RET_DK), lambda b: (0, 0)),
        pl.BlockSpec((RET_HEADS, S_PAD, RET_DV), lambda b: (0, 0, 0)),
        pl.BlockSpec((RET_HEADS, S_PAD, RET_DK), lambda b: (0, 0, 0)),
    ]
    args = [cdec, state_all, q16, k16, v16, cos16, sin16, qdec16, kdec16]
    aliases = {}
    kern = _ret_sample_kernel
    if sn_prev is not None:
        in_specs.append(pl.BlockSpec(memory_space=pl.ANY))
        args.append(sn_prev)
        aliases = {len(args) - 1: 1}
        kern = lambda *refs: _ret_sample_kernel(*refs[:9], *refs[10:])
    return pl.pallas_call(
        kern,
        grid=(DEC_BATCH,),
        in_specs=in_specs,
        out_specs=[
            pl.BlockSpec((None, DEC_SEQ, 2 * D_MODEL), lambda b: (b, 0, 0)),
            pl.BlockSpec(st_block, lambda b: (j, b, 0, 0, 0)),
        ],
        out_shape=[
            jax.ShapeDtypeStruct((DEC_BATCH, DEC_SEQ, 2 * D_MODEL), F32),
            jax.ShapeDtypeStruct(state_all.shape, F32),
        ],
        input_output_aliases=aliases,
        compiler_params=_cparams(1),
        name="ret_sample",
    )(*args)


def _ret_chunk_kernel(cd_ref, qk_ref, v_ref, g_ref, cos_ref, sin_ref, dm_ref, qd_ref, kd_ref, oc_ref,
                      o_ref, st_ref, s_ref):
    i = pl.program_id(0)
    is_s = i >= N_CHUNKS_P
    n = i % CHUNKS_PER_SEQ
    fresh = jnp.logical_or(is_s, n == 0)
    last = jnp.logical_and(jnp.logical_not(is_s), n == CHUNKS_PER_SEQ - 1)

    @pl.when(fresh)
    def _():
        s_ref[...] = jnp.zeros(s_ref.shape, F32)

    cos = cos_ref[...]
    sin = sin_ref[...]
    for h in range(RET_HEADS):
        ks = slice(h * RET_DK, (h + 1) * RET_DK)
        k2 = slice(D_MODEL + h * RET_DK, D_MODEL + (h + 1) * RET_DK)
        vs = slice(h * RET_DV, (h + 1) * RET_DV)
        q = _rotary(qk_ref[:, ks], cos, sin)
        k = _rotary(qk_ref[:, k2], cos, sin) * (RET_DK ** -0.5)
        qb = q.astype(BF16)
        v = v_ref[:, vs]
        s = s_ref[h]
        scores = lax.dot_general(qb, k.astype(BF16), (((1,), (1,)), ((), ())),
                                 preferred_element_type=F32) * dm_ref[h]
        o = jnp.dot(scores.astype(BF16), v, preferred_element_type=F32)
        o = o + jnp.dot(qb, s.astype(BF16), preferred_element_type=F32) * qd_ref[h]
        kd = (k * kd_ref[h]).astype(BF16)
        s_new = s * cd_ref[h] + lax.dot_general(kd, v, (((0,), (0,)), ((), ())),
                                               preferred_element_type=F32)
        s_ref[h] = s_new

        @pl.when(last)
        def _(h=h, s_new=s_new):
            st_ref[h] = s_new

        o = o + jnp.where(is_s, oc_ref[:, vs], 0.0)
        o = o * lax.rsqrt(jnp.mean(o * o, axis=-1, keepdims=True) + EPS)
        o_ref[:, vs] = (jax.nn.silu(g_ref[:, vs]) * o).astype(BF16)


def _ret_chunk(cdec, qk, v, g, cos_all, sin_all, dmask_all, qdec, kdec, ocross):
    def rope_idx(i):
        return jnp.where(i >= N_CHUNKS_P, CHUNKS_PER_SEQ, i % CHUNKS_PER_SEQ)

    return pl.pallas_call(
        _ret_chunk_kernel,
        grid=(N_CHUNKS,),
        in_specs=[
            pl.BlockSpec(memory_space=pltpu.SMEM),
            pl.BlockSpec((CHUNK, 2 * D_MODEL), lambda i: (i, 0)),
            pl.BlockSpec((CHUNK, 2 * D_MODEL), lambda i: (i, 0)),
            pl.BlockSpec((CHUNK, 2 * D_MODEL), lambda i: (i, 0)),
            pl.BlockSpec((CHUNK, RET_DK), lambda i: (rope_idx(i), 0)),
            pl.BlockSpec((CHUNK, RET_DK), lambda i: (rope_idx(i), 0)),
            pl.BlockSpec((None, RET_HEADS, CHUNK, CHUNK), lambda i: (_is_sample_chunk(i), 0, 0, 0)),
            pl.BlockSpec((RET_HEADS, CHUNK, RET_DV), lambda i: (0, 0, 0)),
            pl.BlockSpec((RET_HEADS, CHUNK, RET_DK), lambda i: (0, 0, 0)),
            pl.BlockSpec((CHUNK, 2 * D_MODEL), lambda i: (jnp.maximum(i - N_CHUNKS_P, 0), 0)),
        ],
        out_specs=[
            pl.BlockSpec((CHUNK, 2 * D_MODEL), lambda i: (i, 0)),
            pl.BlockSpec((None, RET_HEADS, RET_DK, RET_DV),
                         lambda i: (jnp.minimum(i // CHUNKS_PER_SEQ, BATCH - 1), 0, 0, 0)),
        ],
        out_shape=[
            jax.ShapeDtypeStruct((M_ALL, 2 * D_MODEL), BF16),
            jax.ShapeDtypeStruct((BATCH, RET_HEADS, RET_DK, RET_DV), F32),
        ],
        scratch_shapes=[pltpu.VMEM((RET_HEADS, RET_DK, RET_DV), F32)],
        compiler_params=_cparams(1),
        name="ret_chunk",
    )(cdec, qk, v, g, cos_all, sin_all, dmask_all, qdec, kdec, ocross)


def _rope_tables():
    theta = 1.0 / (ROPE_BASE ** jnp.linspace(0.0, 1.0, RET_DK // 2, dtype=F32))
    sign = jnp.tile(jnp.array([-1.0, 1.0], F32), RET_DK // 2)

    def tab(pos):
        ang = pos[:, None] * theta[None, :]
        return jnp.repeat(jnp.cos(ang), 2, axis=-1), jnp.repeat(jnp.sin(ang), 2, axis=-1) * sign

    cos_p, sin_p = tab(0.0 + jnp.arange(SEQ, dtype=F32))
    cos_s, sin_s = tab(float(PAST_LEN) + jnp.arange(DEC_SEQ, dtype=F32))
    reps = CHUNK // DEC_SEQ
    cos_all = jnp.concatenate([cos_p, jnp.tile(cos_s, (reps, 1))], axis=0)
    sin_all = jnp.concatenate([sin_p, jnp.tile(sin_s, (reps, 1))], axis=0)
    pad = ((0, S_PAD - DEC_SEQ), (0, 0))
    return cos_all, sin_all, jnp.pad(cos_s, pad), jnp.pad(sin_s, pad)


def _decay_tables(c):
    lg = jnp.log(1.0 - 2.0 ** (-5.0 - jnp.arange(RET_HEADS, dtype=F32)))
    idx = jnp.arange(c, dtype=F32)
    diff = idx[:, None] - idx[None, :]
    dmask = jnp.where(diff[None] >= 0, jnp.exp(lg[:, None, None] * jnp.maximum(diff, 0.0)[None]), 0.0)
    q_dec = jnp.exp(lg[:, None] * (idx + 1.0))
    k_dec = jnp.exp(lg[:, None] * (c - 1.0 - idx))
    c_dec = jnp.exp(lg * c)
    return dmask, q_dec, k_dec, c_dec


def kernel(x_prompt, x_sample, state_ret, c_prompt, c_sample, norm_g, w_ada, b_ada, ffn1_w_gu, ffn1_w_down, ffn2_w_gu, ffn2_w_down, sgu_w_in, sgu_ln_g, sgu_ln_b, sgu_w_s, sgu_b_s, sgu_w_out, ret_w_in, ret_w_out, final_norm_g, w_ada_f, b_ada_f):
    d = D_MODEL
    x = jnp.concatenate([x_prompt.reshape(M_P, d), x_sample.reshape(M_S, d)], axis=0)

    c_all = jnp.concatenate([c_prompt, c_sample], axis=0)
    c_pad = jnp.pad(c_all, ((0, C_PAD - c_all.shape[0]), (0, 0)))
    mod = _ada(c_pad, w_ada, b_ada, 1024)
    mod_f = _ada(c_pad, w_ada_f[None], b_ada_f[None], 1024)

    def split_mod(m):
        n_l, _, n = m.shape
        return (m[:, :BATCH].reshape(n_l, BATCH, 1, n),
                jnp.repeat(m[:, BATCH:BATCH + DEC_BATCH], DEC_SEQ, axis=1))

    mod_p, mod_s = split_mod(mod)
    modf_p, modf_s = split_mod(mod_f)
    gains = norm_g.reshape(DEPTH * 3, 1, d)

    cos_all, sin_all, cos16, sin16 = _rope_tables()
    dmask_p, qdec_p, kdec_p, cdec_p = _decay_tables(RET_CHUNK)
    dmask_4, qdec_4, kdec_4, cdec_4 = _decay_tables(DEC_SEQ)
    eye = jnp.eye(CHUNK // DEC_SEQ, dtype=F32)
    dmask_s = jax.vmap(lambda m: jnp.kron(eye, m))(dmask_4)
    dmask_all = jnp.stack([dmask_p, dmask_s])
    qdec_b = jnp.broadcast_to(qdec_p[:, :, None], (RET_HEADS, CHUNK, RET_DV))
    kdec_b = jnp.broadcast_to(kdec_p[:, :, None], (RET_HEADS, CHUNK, RET_DK))
    pad4 = ((0, 0), (0, S_PAD - DEC_SEQ))
    qdec16 = jnp.broadcast_to(jnp.pad(qdec_4, pad4)[:, :, None], (RET_HEADS, S_PAD, RET_DV))
    kdec16 = jnp.broadcast_to(jnp.pad(kdec_4, pad4)[:, :, None], (RET_HEADS, S_PAD, RET_DK))

    def ffn(x, l, k_norm, w_gu, w_down, c0):
        h = _norm_mod(x, gains, 3 * l + k_norm, mod_p, mod_s, l, c0, c0 + 1, BF16)
        hid = _mm_swiglu(h, w_gu, l, 512)
        return _mm_res(hid, w_down, l, x, mod_p, mod_s, l, c0 + 2, 0.5, 512, "ffn_down")

    ret_p, vn_s = [], []
    sn_all = None
    for l in range(DEPTH):
        x = ffn(x, l, 0, ffn1_w_gu, ffn1_w_down, 0)
        h = _norm_mod(x, gains, 3 * l + 1, mod_p, mod_s, l, 3, 4, BF16)
        j = l // 2
        if l % 2 == 0:
            u = _mm_act(h, sgu_w_in, j, 0, D_SGU, 1024, _gelu_tanh, BF16, "sgu_in_u")
            v = _mm_act(h, sgu_w_in, j, D_SGU // 1024, D_SGU, 1024, _gelu_tanh, F32, "sgu_in_v")
            ws = sgu_w_s[j]
            wm_s = jax.vmap(lambda m: jnp.kron(eye, m))(jnp.tril(ws[:, :DEC_SEQ, :DEC_SEQ]))
            wm_all = jnp.stack([ws, wm_s])
            bs = sgu_b_s[j]
            bias_p = jnp.repeat(bs.T, SGU_GROUP_DIM, axis=1)
            bias_s = jnp.repeat(jnp.tile(bs[:, :DEC_SEQ].T, (CHUNK // DEC_SEQ, 1)), SGU_GROUP_DIM, axis=1)
            bias_all = jnp.stack([bias_p, bias_s])
            gated, vn = _sgu_mix(u, v, sgu_ln_g.reshape(N_A, 1, D_SGU), sgu_ln_b.reshape(N_A, 1, D_SGU),
                                 j, wm_all, bias_all)
            vn_s.append(vn.reshape(DEC_BATCH, DEC_SEQ, D_SGU))
            x = _mm_res(gated, sgu_w_out, j, x, mod_p, mod_s, l, 5, 1.0, 512, "sgu_out")
        else:
            qk = _mm_act(h, ret_w_in, j, 0, 2 * d, 1024, None, F32, "ret_in_qk")
            v = _mm_act(h, ret_w_in, j, 4, 2 * d, 1024, None, BF16, "ret_in_v")
            g = _mm_act(h, ret_w_in, j, 8, 2 * d, 1024, None, F32, "ret_in_g")
            pad_rows = ((0, 0), (0, S_PAD - DEC_SEQ), (0, 0))
            q16 = jnp.pad(qk[M_P:, :d].reshape(DEC_BATCH, DEC_SEQ, d), pad_rows)
            k16 = jnp.pad(qk[M_P:, d:].reshape(DEC_BATCH, DEC_SEQ, d), pad_rows)
            v16 = jnp.pad(v[M_P:].reshape(DEC_BATCH, DEC_SEQ, 2 * d), pad_rows)
            ocross, sn_all = _ret_sample(cdec_4, state_ret, j, q16, k16, v16, cos16, sin16,
                                         qdec16, kdec16, sn_all)
            gated, st_p = _ret_chunk(cdec_p, qk, v, g, cos_all, sin_all, dmask_all, qdec_b, kdec_b,
                                     ocross.reshape(M_S, 2 * d))
            ret_p.append(st_p)
            x = _mm_res(gated, ret_w_out, j, x, mod_p, mod_s, l, 5, 1.0, 512, "ret_out")
        x = ffn(x, l, 2, ffn2_w_gu, ffn2_w_down, 6)

    y = _norm_mod(x, final_norm_g.reshape(1, 1, d), 0, modf_p, modf_s, 0, 0, 1, F32)
    y_prompt = y[:M_P].reshape(BATCH, SEQ, d)
    y_sample = y[M_P:].reshape(DEC_BATCH, DEC_SEQ, d)
    return (y_prompt, y_sample, jnp.stack(ret_p), sn_all, jnp.stack(vn_s))
```

```python
import functools

import jax
import jax.numpy as jnp
from jax import lax
from jax.experimental import pallas as pl
from jax.experimental.pallas import tpu as pltpu

F32 = jnp.float32
BF16 = jnp.bfloat16

D_MODEL = 2048
BATCH = 4
SEQ = 2048
DEPTH = 4
DEC_BATCH = 128
DEC_SEQ = 4
PAST_LEN = 16384
N_A = 2
N_B = 2
SGU_CHUNK = 128
SGU_GROUPS = 8
D_SGU = 3 * D_MODEL
SGU_GROUP_DIM = D_SGU // SGU_GROUPS
RET_HEADS = 8
RET_DK = D_MODEL // RET_HEADS
RET_DV = 2 * D_MODEL // RET_HEADS
ROPE_BASE = 10000.0
D_FF = 5632
N_ADA = 9
EPS = 1e-6

M_P = BATCH * SEQ
M_S = DEC_BATCH * DEC_SEQ
M_ALL = M_P + M_S

TM_NORM = 512
N_NORM_TILES = M_ALL // TM_NORM
NORM_ROWS = 16
NORM_UNROLL = 8
TM_IN = M_ALL // 8
TN_IN = 1024
TF_GU = 512
TM_RES = 256
TN_RES = 1024
RES_TILES_PER_SEQ = SEQ // TM_RES
RES_TILES_P = M_P // TM_RES
N_RES_TILES = M_ALL // TM_RES
TN_ADA = 1024
LANES = 128
CAST_ROWS = 256
SCHUNK = SGU_CHUNK
N_SC_P = M_P // SCHUNK
N_SC = M_ALL // SCHUNK
RCHUNK = 256
N_RC_P = M_P // RCHUNK
N_RC = M_ALL // RCHUNK
RC_PER_SEQ = SEQ // RCHUNK
C_PAD = 144
C_ROWS = M_S + (C_PAD - DEC_BATCH)
S_PAD = 16
VMEM_LIMIT = 58 * 1024 * 1024


def _cparams(n_axes):
    return pltpu.CompilerParams(
        dimension_semantics=("arbitrary",) * n_axes, vmem_limit_bytes=VMEM_LIMIT)


def _gelu_tanh(x):
    return jax.nn.gelu(x, approximate=True)


def _ada_kernel(c_ref, w_ref, b_ref, o_ref, a_ref, e_ref):
    @pl.when(jnp.logical_and(pl.program_id(0) == 0, pl.program_id(1) == 0))
    def _():
        a_ref[...] = jax.nn.silu(c_ref[...]).astype(BF16)

    w = w_ref[...].astype(BF16)
    y = jnp.dot(a_ref[...], w, preferred_element_type=F32) + b_ref[...]
    for c in range(TN_ADA // LANES):
        sl = slice(c * LANES, (c + 1) * LANES)
        for t in range(DEC_SEQ):
            e_ref[c, pl.ds(t, DEC_BATCH, stride=DEC_SEQ), :] = y[:DEC_BATCH, sl]
        e_ref[c, M_S:, :] = y[DEC_BATCH:, sl]
        o_ref[:, sl] = e_ref[c]


def _ada(c_pad, w, b):
    n_l, k, n = w.shape
    return pl.pallas_call(
        _ada_kernel,
        grid=(n_l, n // TN_ADA),
        in_specs=[
            pl.BlockSpec((C_PAD, k), lambda l, j: (0, 0)),
            pl.BlockSpec((None, k, TN_ADA), lambda l, j: (l, 0, j)),
            pl.BlockSpec((None, 1, TN_ADA), lambda l, j: (l, 0, j)),
        ],
        out_specs=pl.BlockSpec((None, C_ROWS, TN_ADA), lambda l, j: (l, 0, j)),
        out_shape=jax.ShapeDtypeStruct((n_l, C_ROWS, n), F32),
        scratch_shapes=[pltpu.VMEM((C_PAD, k), BF16), pltpu.VMEM((TN_ADA // LANES, C_ROWS, LANES), F32)],
        compiler_params=_cparams(2),
        name="ada",
    )(c_pad, w, b.reshape(n_l, 1, n))


def _norm_kernel(x_ref, g_ref, shp_ref, scp_ref, shs_ref, scs_ref, *o_refs):
    op_ref, os_ref = (o_refs[0], o_refs[-1])
    g = g_ref[...]
    is_s = pl.program_id(0) == N_NORM_TILES - 1

    def normed(r0):
        x = x_ref[pl.ds(r0, NORM_ROWS), :]
        return (x * lax.rsqrt(jnp.mean(x * x, axis=-1, keepdims=True) + EPS)) * g

    def each_row_block(fn):
        def body(r, carry):
            fn(pl.multiple_of(r * NORM_ROWS, NORM_ROWS))
            return carry
        lax.fori_loop(0, TM_NORM // NORM_ROWS, body, 0, unroll=NORM_UNROLL)

    @pl.when(jnp.logical_not(is_s))
    def _():
        shift = shp_ref[...]
        scale1 = 1.0 + scp_ref[...]

        def fn(r0):
            op_ref[pl.ds(r0, NORM_ROWS), :] = (normed(r0) * scale1 + shift).astype(op_ref.dtype)
        each_row_block(fn)

    @pl.when(is_s)
    def _():
        def fn(r0):
            rows = pl.ds(r0, NORM_ROWS)
            os_ref[rows, :] = (normed(r0) * (1.0 + scs_ref[rows, :]) + shs_ref[rows, :]).astype(os_ref.dtype)
        each_row_block(fn)


def _norm_mod(x, gains, g_idx, mod_p, mod, l, c_shift, c_scale, out_dtype, split):
    d = D_MODEL
    if split:
        out_specs = [pl.BlockSpec((TM_NORM, d), lambda i: (jnp.minimum(i, N_NORM_TILES - 2), 0)),
                     pl.BlockSpec((TM_NORM, d), lambda i: (0, 0))]
        out_shape = [jax.ShapeDtypeStruct((M_P, d), out_dtype), jax.ShapeDtypeStruct((M_S, d), out_dtype)]
    else:
        out_specs = pl.BlockSpec((TM_NORM, d), lambda i: (i, 0))
        out_shape = jax.ShapeDtypeStruct((M_ALL, d), out_dtype)
    tiles_per_seq = SEQ // TM_NORM

    def seq(i):
        return jnp.minimum(i // tiles_per_seq, BATCH - 1)

    return pl.pallas_call(
        _norm_kernel,
        grid=(N_NORM_TILES,),
        in_specs=[
            pl.BlockSpec((TM_NORM, d), lambda i: (i, 0)),
            pl.BlockSpec((None, 1, d), lambda i: (g_idx, 0, 0)),
            pl.BlockSpec((None, None, 1, d), lambda i: (l, seq(i), 0, c_shift)),
            pl.BlockSpec((None, None, 1, d), lambda i: (l, seq(i), 0, c_scale)),
            pl.BlockSpec((None, TM_NORM, d), lambda i: (l, 0, c_shift)),
            pl.BlockSpec((None, TM_NORM, d), lambda i: (l, 0, c_scale)),
        ],
        out_specs=out_specs,
        out_shape=out_shape,
        compiler_params=_cparams(1),
        name="norm_mod",
    )(x, gains, mod_p, mod_p, mod, mod)


def _round_weights(w_refs, wb_refs):
    @pl.when(pl.program_id(1) == 0)
    def _():
        for w_ref, wb_ref in zip(w_refs, wb_refs):
            def body(c, carry, w_ref=w_ref, wb_ref=wb_ref):
                r = pl.multiple_of(c * CAST_ROWS, CAST_ROWS)
                wb_ref[pl.ds(r, CAST_ROWS), :] = w_ref[pl.ds(r, CAST_ROWS), :].astype(BF16)
                return carry
            lax.fori_loop(0, w_ref.shape[0] // CAST_ROWS, body, 0)


def _rotary(x, cos, sin_signed):
    w = x.shape[-1]
    lane = lax.broadcasted_iota(jnp.int32, x.shape, x.ndim - 1)
    nxt = pltpu.roll(x, w - 1, x.ndim - 1)
    prv = pltpu.roll(x, 1, x.ndim - 1)
    swapped = jnp.where((lane & 1) == 0, nxt, prv)
    return x * cos + swapped * sin_signed


def _mm_act_kernel(a_ref, w_ref, o_ref, wb_ref, *, act):
    _round_weights([w_ref], [wb_ref])
    y = jnp.dot(a_ref[...], wb_ref[...], preferred_element_type=F32)
    if act is not None:
        y = act(y)
    o_ref[...] = y.astype(o_ref.dtype)


def _mm_swiglu_kernel(a_ref, wg_ref, wu_ref, o_ref, wgb_ref, wub_ref):
    _round_weights([wg_ref, wu_ref], [wgb_ref, wub_ref])
    a = a_ref[...]
    gt = jnp.dot(a, wgb_ref[...], preferred_element_type=F32)
    up = jnp.dot(a, wub_ref[...], preferred_element_type=F32)
    o_ref[...] = (jax.nn.silu(gt) * up).astype(o_ref.dtype)


def _mm_rope_kernel(a_ref, w_ref, cos_ref, sin_ref, o_ref, wb_ref):
    _round_weights([w_ref], [wb_ref])
    y = jnp.dot(a_ref[...], wb_ref[...], preferred_element_type=F32)
    scale = jnp.where(pl.program_id(0) >= D_MODEL // TN_IN, RET_DK ** -0.5, 1.0)
    cos = cos_ref[...]
    sin = sin_ref[...]
    for hh in range(TN_IN // RET_DK):
        sl = slice(hh * RET_DK, (hh + 1) * RET_DK)
        o_ref[:, sl] = (_rotary(y[:, sl], cos, sin) * scale).astype(BF16)


def _mm_res_kernel(a_ref, w_ref, x_ref, gp_ref, gs_ref, o_ref, wb_ref, *, coef):
    _round_weights([w_ref], [wb_ref])
    y = jnp.dot(a_ref[...], wb_ref[...], preferred_element_type=F32)
    is_s = pl.program_id(1) >= RES_TILES_P
    gate = jnp.where(is_s, gs_ref[...], gp_ref[...])
    o_ref[...] = x_ref[...] + (coef * (1.0 + gate)) * y


def _in_grid(n_cols):
    return (n_cols // TN_IN, M_ALL // TM_IN)


def _mm_act(a, w, l, col0, n_cols, act, name):
    k = a.shape[1]
    c0 = col0 // TN_IN
    return pl.pallas_call(
        functools.partial(_mm_act_kernel, act=act),
        grid=_in_grid(n_cols),
        in_specs=[
            pl.BlockSpec((TM_IN, k), lambda j, i: (i, 0)),
            pl.BlockSpec((None, k, TN_IN), lambda j, i: (l, 0, c0 + j)),
        ],
        out_specs=pl.BlockSpec((TM_IN, TN_IN), lambda j, i: (i, j)),
        out_shape=jax.ShapeDtypeStruct((M_ALL, n_cols), BF16),
        scratch_shapes=[pltpu.VMEM((k, TN_IN), BF16)],
        compiler_params=_cparams(2),
        name=name,
    )(a, w)


def _mm_rope(a, w, l, cos_rows, sin_rows):
    k = a.shape[1]
    n_cols = 2 * D_MODEL
    return pl.pallas_call(
        _mm_rope_kernel,
        grid=_in_grid(n_cols),
        in_specs=[
            pl.BlockSpec((TM_IN, k), lambda j, i: (i, 0)),
            pl.BlockSpec((None, k, TN_IN), lambda j, i: (l, 0, j)),
            pl.BlockSpec((TM_IN, RET_DK), lambda j, i: (i, 0)),
            pl.BlockSpec((TM_IN, RET_DK), lambda j, i: (i, 0)),
        ],
        out_specs=pl.BlockSpec((TM_IN, TN_IN), lambda j, i: (i, j)),
        out_shape=jax.ShapeDtypeStruct((M_ALL, n_cols), BF16),
        scratch_shapes=[pltpu.VMEM((k, TN_IN), BF16)],
        compiler_params=_cparams(2),
        name="ret_in_qk",
    )(a, w, cos_rows, sin_rows)


def _mm_swiglu(a, w_gu, l):
    k = a.shape[1]
    n_f = D_FF // TF_GU
    return pl.pallas_call(
        _mm_swiglu_kernel,
        grid=(n_f, M_ALL // TM_IN),
        in_specs=[
            pl.BlockSpec((TM_IN, k), lambda j, i: (i, 0)),
            pl.BlockSpec((None, k, TF_GU), lambda j, i: (l, 0, j)),
            pl.BlockSpec((None, k, TF_GU), lambda j, i: (l, 0, n_f + j)),
        ],
        out_specs=pl.BlockSpec((TM_IN, TF_GU), lambda j, i: (i, j)),
        out_shape=jax.ShapeDtypeStruct((M_ALL, D_FF), BF16),
        scratch_shapes=[pltpu.VMEM((k, TF_GU), BF16), pltpu.VMEM((k, TF_GU), BF16)],
        compiler_params=_cparams(2),
        name="ffn_gu",
    )(a, w_gu, w_gu)


def _mm_res(a, w, wl, x, mod_p, mod, l, c_gate, coef, name):
    k = a.shape[1]
    d = D_MODEL
    nt = d // TN_RES

    def seq(i):
        return jnp.minimum(i // RES_TILES_PER_SEQ, BATCH - 1)

    return pl.pallas_call(
        functools.partial(_mm_res_kernel, coef=coef),
        grid=(nt, N_RES_TILES),
        in_specs=[
            pl.BlockSpec((TM_RES, k), lambda j, i: (i, 0)),
            pl.BlockSpec((None, k, TN_RES), lambda j, i: (wl, 0, j), pipeline_mode=pl.Buffered(1)),
            pl.BlockSpec((TM_RES, TN_RES), lambda j, i: (i, j)),
            pl.BlockSpec((None, None, 1, TN_RES), lambda j, i: (l, seq(i), 0, c_gate * nt + j)),
            pl.BlockSpec((None, TM_RES, TN_RES),
                         lambda j, i: (l, jnp.maximum(i - RES_TILES_P, 0), c_gate * nt + j)),
        ],
        out_specs=pl.BlockSpec((TM_RES, TN_RES), lambda j, i: (i, j)),
        out_shape=jax.ShapeDtypeStruct((M_ALL, d), F32),
        scratch_shapes=[pltpu.VMEM((k, TN_RES), BF16)],
        compiler_params=_cparams(2),
        name=name,
    )(a, w, x, mod_p, mod)


def _sgu_kernel(u_ref, v_ref, lg_ref, lb_ref, wm_ref, bias_ref, o_ref, vn_ref):
    v = v_ref[...].astype(F32)
    mu = jnp.mean(v, axis=-1, keepdims=True)
    vc = v - mu
    vn = vc * lax.rsqrt(jnp.mean(vc * vc, axis=-1, keepdims=True) + EPS)
    vn = vn * lg_ref[...] + lb_ref[...]
    vn_ref[...] = vn
    vnb = vn.astype(BF16)
    row = lax.broadcasted_iota(jnp.int32, (SCHUNK, SCHUNK), 0)
    col = lax.broadcasted_iota(jnp.int32, (SCHUNK, SCHUNK), 1)
    causal = row >= col
    for g in range(SGU_GROUPS):
        sl = slice(g * SGU_GROUP_DIM, (g + 1) * SGU_GROUP_DIM)
        wm = jnp.where(causal, wm_ref[g], 0.0).astype(BF16)
        mixed = jnp.dot(wm, vnb[:, sl], preferred_element_type=F32) + bias_ref[:, sl]
        o_ref[:, sl] = (u_ref[:, sl].astype(F32) * mixed).astype(BF16)


def _sgu_mix(uv, ln_g, ln_b, j, wm_all, bias_all):
    n = D_SGU

    def kind(i):
        return (i >= N_SC_P).astype(jnp.int32)

    return pl.pallas_call(
        _sgu_kernel,
        grid=(N_SC,),
        in_specs=[
            pl.BlockSpec((SCHUNK, n), lambda i: (i, 0)),
            pl.BlockSpec((SCHUNK, n), lambda i: (i, 1)),
            pl.BlockSpec((None, 1, n), lambda i: (j, 0, 0)),
            pl.BlockSpec((None, 1, n), lambda i: (j, 0, 0)),
            pl.BlockSpec((None, SGU_GROUPS, SCHUNK, SCHUNK), lambda i: (kind(i), 0, 0, 0)),
            pl.BlockSpec((None, SCHUNK, n), lambda i: (kind(i), 0, 0)),
        ],
        out_specs=[
            pl.BlockSpec((SCHUNK, n), lambda i: (i, 0)),
            pl.BlockSpec((SCHUNK, n), lambda i: (jnp.maximum(i - N_SC_P, 0), 0)),
        ],
        out_shape=[
            jax.ShapeDtypeStruct((M_ALL, n), BF16),
            jax.ShapeDtypeStruct((M_S, n), F32),
        ],
        compiler_params=_cparams(1),
        name="sgu_mix",
    )(uv, uv, ln_g, ln_b, wm_all, bias_all)


_NT = (((1,), (1,)), ((), ()))
_TN = (((0,), (0,)), ((), ()))


def _ret_sample_kernel(cd_ref, s0_ref, q_ref, k_ref, v_ref, qd_ref, kd_ref, oc_ref, sn_ref):
    for h in range(RET_HEADS):
        ks = slice(h * RET_DK, (h + 1) * RET_DK)
        vs = slice(h * RET_DV, (h + 1) * RET_DV)
        s0 = s0_ref[h]
        oc = jnp.dot(q_ref[:, ks], s0.astype(BF16), preferred_element_type=F32) * qd_ref[h]
        oc_ref[:, vs] = oc[:DEC_SEQ]
        kd = (k_ref[:, ks].astype(F32) * kd_ref[h]).astype(BF16)
        kv = lax.dot_general(kd, v_ref[:, vs], _TN, preferred_element_type=F32)
        sn_ref[h] = s0 * cd_ref[h] + kv


def _ret_sample(cdec, state_all, j, q16, k16, v16, qdec16, kdec16, sn_prev):
    st_block = (None, None, RET_HEADS, RET_DK, RET_DV)
    in_specs = [
        pl.BlockSpec(memory_space=pltpu.SMEM),
        pl.BlockSpec(st_block, lambda b: (j, b, 0, 0, 0)),
        pl.BlockSpec((None, S_PAD, D_MODEL), lambda b: (b, 0, 0)),
        pl.BlockSpec((None, S_PAD, D_MODEL), lambda b: (b, 0, 0)),
        pl.BlockSpec((None, S_PAD, 2 * D_MODEL), lambda b: (b, 0, 0)),
        pl.BlockSpec((RET_HEADS, S_PAD, RET_DV), lambda b: (0, 0, 0)),
        pl.BlockSpec((RET_HEADS, S_PAD, RET_DK), lambda b: (0, 0, 0)),
    ]
    args = [cdec, state_all, q16, k16, v16, qdec16, kdec16]
    n_in = len(args)
    aliases = {}
    kern = _ret_sample_kernel
    if sn_prev is not None:
        in_specs.append(pl.BlockSpec(memory_space=pl.ANY))
        args.append(sn_prev)
        aliases = {n_in: 1}
        kern = lambda *refs: _ret_sample_kernel(*refs[:n_in], *refs[n_in + 1:])
    return pl.pallas_call(
        kern,
        grid=(DEC_BATCH,),
        in_specs=in_specs,
        out_specs=[
            pl.BlockSpec((None, DEC_SEQ, 2 * D_MODEL), lambda b: (b, 0, 0)),
            pl.BlockSpec(st_block, lambda b: (j, b, 0, 0, 0)),
        ],
        out_shape=[
            jax.ShapeDtypeStruct((DEC_BATCH, DEC_SEQ, 2 * D_MODEL), F32),
            jax.ShapeDtypeStruct(state_all.shape, F32),
        ],
        input_output_aliases=aliases,
        compiler_params=_cparams(1),
        name="ret_sample",
    )(*args)


def _ret_chunk_kernel(cd_ref, q_ref, k_ref, v_ref, gate_ref, dm_ref, qd_ref, kd_ref, oc_ref,
                      o_ref, st_ref):
    i = pl.program_id(0)
    is_s = i >= N_RC_P
    first = jnp.logical_and(jnp.logical_not(is_s), i % RC_PER_SEQ == 0)

    def finish(h, o):
        vs = slice(h * RET_DV, (h + 1) * RET_DV)
        o = o * lax.rsqrt(jnp.mean(o * o, axis=-1, keepdims=True) + EPS)
        o_ref[:, vs] = (gate_ref[:, vs].astype(F32) * o).astype(BF16)

    def intra(h):
        ks = slice(h * RET_DK, (h + 1) * RET_DK)
        vs = slice(h * RET_DV, (h + 1) * RET_DV)
        q = q_ref[:, ks]
        k = k_ref[:, ks]
        v = v_ref[:, vs]
        scores = lax.dot_general(q, k, _NT, preferred_element_type=F32) * dm_ref[h]
        return q, k, v, jnp.dot(scores.astype(BF16), v, preferred_element_type=F32)

    @pl.when(first)
    def _():
        st_ref[...] = jnp.zeros(st_ref.shape, F32)

    @pl.when(jnp.logical_not(is_s))
    def _():
        for h in range(RET_HEADS):
            q, k, v, o = intra(h)
            s = st_ref[h]
            o = o + jnp.dot(q, s.astype(BF16), preferred_element_type=F32) * qd_ref[h]
            kd = (k.astype(F32) * kd_ref[h]).astype(BF16)
            st_ref[h] = s * cd_ref[h] + lax.dot_general(kd, v, _TN, preferred_element_type=F32)
            finish(h, o)

    @pl.when(is_s)
    def _():
        for h in range(RET_HEADS):
            _, _, _, o = intra(h)
            finish(h, o + oc_ref[:, h * RET_DV:(h + 1) * RET_DV])


def _ret_chunk(cdec, qk, v, gate, dmask_all, qdec, kdec, ocross):
    d = D_MODEL
    const = dict(pipeline_mode=pl.Buffered(1))
    return pl.pallas_call(
        _ret_chunk_kernel,
        grid=(N_RC,),
        in_specs=[
            pl.BlockSpec(memory_space=pltpu.SMEM),
            pl.BlockSpec((RCHUNK, d), lambda i: (i, 0)),
            pl.BlockSpec((RCHUNK, d), lambda i: (i, 1)),
            pl.BlockSpec((RCHUNK, 2 * d), lambda i: (i, 0)),
            pl.BlockSpec((RCHUNK, 2 * d), lambda i: (i, 0)),
            pl.BlockSpec((None, RET_HEADS, RCHUNK, RCHUNK),
                         lambda i: ((i >= N_RC_P).astype(jnp.int32), 0, 0, 0)),
            pl.BlockSpec((RET_HEADS, RCHUNK, RET_DV), lambda i: (0, 0, 0), **const),
            pl.BlockSpec((RET_HEADS, RCHUNK, RET_DK), lambda i: (0, 0, 0), **const),
            pl.BlockSpec((RCHUNK, 2 * d), lambda i: (jnp.maximum(i - N_RC_P, 0), 0)),
        ],
        out_specs=[
            pl.BlockSpec((RCHUNK, 2 * d), lambda i: (i, 0)),
            pl.BlockSpec((None, RET_HEADS, RET_DK, RET_DV),
                         lambda i: (jnp.minimum(i // RC_PER_SEQ, BATCH - 1), 0, 0, 0)),
        ],
        out_shape=[
            jax.ShapeDtypeStruct((M_ALL, 2 * d), BF16),
            jax.ShapeDtypeStruct((BATCH, RET_HEADS, RET_DK, RET_DV), F32),
        ],
        compiler_params=_cparams(1),
        name="ret_chunk",
    )(cdec, qk, qk, v, gate, dmask_all, qdec, kdec, ocross)


def _rope_rows():
    theta = 1.0 / (ROPE_BASE ** jnp.linspace(0.0, 1.0, RET_DK // 2, dtype=F32))
    sign = jnp.tile(jnp.array([-1.0, 1.0], F32), RET_DK // 2)

    def tab(pos):
        ang = pos[:, None] * theta[None, :]
        return jnp.repeat(jnp.cos(ang), 2, axis=-1), jnp.repeat(jnp.sin(ang), 2, axis=-1) * sign

    cos_p, sin_p = tab(0.0 + jnp.arange(SEQ, dtype=F32))
    cos_s, sin_s = tab(float(PAST_LEN) + jnp.arange(DEC_SEQ, dtype=F32))
    cos_rows = jnp.concatenate([jnp.tile(cos_p, (BATCH, 1)), jnp.tile(cos_s, (DEC_BATCH, 1))], axis=0)
    sin_rows = jnp.concatenate([jnp.tile(sin_p, (BATCH, 1)), jnp.tile(sin_s, (DEC_BATCH, 1))], axis=0)
    return cos_rows, sin_rows


def _decay_tables(c):
    lg = jnp.log(1.0 - 2.0 ** (-5.0 - jnp.arange(RET_HEADS, dtype=F32)))
    idx = jnp.arange(c, dtype=F32)
    diff = idx[:, None] - idx[None, :]
    dmask = jnp.where(diff[None] >= 0, jnp.exp(lg[:, None, None] * jnp.maximum(diff, 0.0)[None]), 0.0)
    q_dec = jnp.exp(lg[:, None] * (idx + 1.0))
    k_dec = jnp.exp(lg[:, None] * (c - 1.0 - idx))
    c_dec = jnp.exp(lg * c)
    return dmask, q_dec, k_dec, c_dec


def _block_diag(blocks, reps):
    eye = jnp.eye(reps, dtype=F32)
    return jax.vmap(lambda m: jnp.kron(eye, m))(blocks)


def kernel(x_prompt, x_sample, state_ret, c_prompt, c_sample, norm_g, w_ada, b_ada, ffn1_w_gu, ffn1_w_down, ffn2_w_gu, ffn2_w_down, sgu_w_in, sgu_ln_g, sgu_ln_b, sgu_w_s, sgu_b_s, sgu_w_out, ret_w_in, ret_w_out, final_norm_g, w_ada_f, b_ada_f):
    d = D_MODEL
    x = jnp.concatenate([x_prompt.reshape(M_P, d), x_sample.reshape(M_S, d)], axis=0)

    c_pad = jnp.pad(jnp.concatenate([c_sample, c_prompt], axis=0), ((0, C_PAD - DEC_BATCH - BATCH), (0, 0)))
    mod = _ada(c_pad, w_ada, b_ada)
    mod_f = _ada(c_pad, w_ada_f[None], b_ada_f[None])

    def prompt_rows(m):
        return m[:, M_S:M_S + BATCH].reshape(m.shape[0], BATCH, 1, m.shape[2])

    mod_p = prompt_rows(mod)
    modf_p = prompt_rows(mod_f)
    gains = norm_g.reshape(DEPTH * 3, 1, d)

    cos_rows, sin_rows = _rope_rows()
    dmask_p, qdec_p, kdec_p, cdec_p = _decay_tables(RCHUNK)
    dmask_4, qdec_4, kdec_4, cdec_4 = _decay_tables(DEC_SEQ)
    dmask_all = jnp.stack([dmask_p, _block_diag(dmask_4, RCHUNK // DEC_SEQ)])
    qdec_b = jnp.broadcast_to(qdec_p[:, :, None], (RET_HEADS, RCHUNK, RET_DV))
    kdec_b = jnp.broadcast_to(kdec_p[:, :, None], (RET_HEADS, RCHUNK, RET_DK))
    pad4 = ((0, 0), (0, S_PAD - DEC_SEQ))
    qdec16 = jnp.broadcast_to(jnp.pad(qdec_4, pad4)[:, :, None], (RET_HEADS, S_PAD, RET_DV))
    kdec16 = jnp.broadcast_to(jnp.pad(kdec_4, pad4)[:, :, None], (RET_HEADS, S_PAD, RET_DK))

    def ffn(x, l, k_norm, w_gu, w_down, c0):
        h = _norm_mod(x, gains, 3 * l + k_norm, mod_p, mod, l, c0, c0 + 1, BF16, False)
        hid = _mm_swiglu(h, w_gu, l)
        return _mm_res(hid, w_down, l, x, mod_p, mod, l, c0 + 2, 0.5, "ffn_down")

    ret_p, vn_s = [], []
    sn_all = None
    for l in range(DEPTH):
        x = ffn(x, l, 0, ffn1_w_gu, ffn1_w_down, 0)
        h = _norm_mod(x, gains, 3 * l + 1, mod_p, mod, l, 3, 4, BF16, False)
        j = l // 2
        if l % 2 == 0:
            uv = _mm_act(h, sgu_w_in, j, 0, 2 * D_SGU, _gelu_tanh, "sgu_in")
            ws = sgu_w_s[j]
            reps = SCHUNK // DEC_SEQ
            wm_all = jnp.stack([ws, _block_diag(jnp.tril(ws[:, :DEC_SEQ, :DEC_SEQ]), reps)])
            bs = sgu_b_s[j]
            bias_p = jnp.repeat(bs.T, SGU_GROUP_DIM, axis=1)
            bias_s = jnp.repeat(jnp.tile(bs[:, :DEC_SEQ].T, (reps, 1)), SGU_GROUP_DIM, axis=1)
            gated, vn = _sgu_mix(uv, sgu_ln_g.reshape(N_A, 1, D_SGU), sgu_ln_b.reshape(N_A, 1, D_SGU),
                                 j, wm_all, jnp.stack([bias_p, bias_s]))
            vn_s.append(vn.reshape(DEC_BATCH, DEC_SEQ, D_SGU))
            x = _mm_res(gated, sgu_w_out, j, x, mod_p, mod, l, 5, 1.0, "sgu_out")
        else:
            qk = _mm_rope(h, ret_w_in, j, cos_rows, sin_rows)
            v = _mm_act(h, ret_w_in, j, 2 * d, 2 * d, None, "ret_in_v")
            gate = _mm_act(h, ret_w_in, j, 4 * d, 2 * d, jax.nn.silu, "ret_in_g")
            pad_rows = ((0, 0), (0, S_PAD - DEC_SEQ), (0, 0))
            qk_s = qk[M_P:].reshape(DEC_BATCH, DEC_SEQ, 2 * d)
            q16 = jnp.pad(qk_s[:, :, :d], pad_rows)
            k16 = jnp.pad(qk_s[:, :, d:], pad_rows)
            v16 = jnp.pad(v[M_P:].reshape(DEC_BATCH, DEC_SEQ, 2 * d), pad_rows)
            ocross, sn_all = _ret_sample(cdec_4, state_ret, j, q16, k16, v16, qdec16, kdec16, sn_all)
            gated, st_p = _ret_chunk(cdec_p, qk, v, gate, dmask_all, qdec_b, kdec_b,
                                     ocross.reshape(M_S, 2 * d))
            ret_p.append(st_p)
            x = _mm_res(gated, ret_w_out, j, x, mod_p, mod, l, 5, 1.0, "ret_out")
        x = ffn(x, l, 2, ffn2_w_gu, ffn2_w_down, 6)

    y_p, y_s = _norm_mod(x, final_norm_g.reshape(1, 1, d), 0, modf_p, mod_f, 0, 0, 1, F32, True)
    return (y_p.reshape(BATCH, SEQ, d), y_s.reshape(DEC_BATCH, DEC_SEQ, d),
            jnp.stack(ret_p), sn_all, jnp.stack(vn_s))
```
